```python
import jax, jax.numpy as jnp
from jax import lax
import numpy as np

D_MODEL = 1024
BATCH = 16
SEQ = 4096
DEPTH = 2
DEC_BATCH = 16
DEC_SEQ = 64
PAST_LEN = 2048

CHUNK = 64
ATT_QBLOCK = 128
N_A_LAYERS = (DEPTH + 1) // 2
N_C_LAYERS = DEPTH // 2
ML_HEADS = 4
ML_DH = D_MODEL // (2 * ML_HEADS)
ML_W = ML_HEADS * ML_DH
RET_HEADS = 4
RET_DK = D_MODEL // (4 * RET_HEADS)
RET_DV = D_MODEL // (2 * RET_HEADS)
RET_KW = RET_HEADS * RET_DK
RET_VW = RET_HEADS * RET_DV
SPLIT_A = (ML_W, ML_W, ML_W, ML_W, 2 * ML_HEADS, RET_KW, RET_KW, RET_VW, RET_VW)
PROJ_A = 4 * ML_W + 2 * ML_HEADS + 2 * RET_KW + 2 * RET_VW
MLA_HEADS = 8
QK_NOPE = 128
QK_ROPE = 64
V_HEAD = 128
Q_LORA = 512
KV_LORA = 256
PROJ_C = Q_LORA + KV_LORA + QK_ROPE
MLA_SCALE = (QK_NOPE + QK_ROPE) ** -0.5
D_FF = -(-8 * D_MODEL // (3 * 256)) * 256
ALPHA = (2 * DEPTH) ** 0.25
BETA = (8 * DEPTH) ** -0.25
ROPE_BASE = 10000.0
LN_EPS = 1e-5
RMS_EPS = 1e-6

kernel_name = 'hybrid_mlstm_retention_mla_stream_step'


def _split(a, sizes):
    idx = np.cumsum(sizes)[:-1].tolist()
    return jnp.split(a, idx, axis=-1)


def layer_norm(x, g, b):
    xf = x.astype(jnp.float32)
    mu = jnp.mean(xf, axis=-1, keepdims=True)
    var = jnp.mean(jnp.square(xf - mu), axis=-1, keepdims=True)
    return ((xf - mu) * lax.rsqrt(var + LN_EPS) * g.astype(jnp.float32) + b.astype(jnp.float32)).astype(x.dtype)


def rms_norm(x, g):
    xf = x.astype(jnp.float32)
    y = xf * lax.rsqrt(jnp.mean(jnp.square(xf), axis=-1, keepdims=True) + RMS_EPS)
    return (y * g.astype(jnp.float32)).astype(x.dtype)


def head_norm(h, g):
    mu = jnp.mean(h, axis=-1, keepdims=True)
    var = jnp.mean(jnp.square(h - mu), axis=-1, keepdims=True)
    return (h - mu) * lax.rsqrt(var + LN_EPS) * g.astype(jnp.float32).reshape(h.shape[-2:])


def rope(x, pos):
    half = x.shape[-1] // 2
    inv = ROPE_BASE ** (-jnp.arange(half, dtype=jnp.float32) / half)
    ang = pos.astype(jnp.float32)[:, None] * inv[None, :]
    cos = jnp.cos(ang)[None, :, None, :]
    sin = jnp.sin(ang)[None, :, None, :]
    xf = x.astype(jnp.float32)
    x1, x2 = xf[..., :half], xf[..., half:]
    return jnp.concatenate([x1 * cos - x2 * sin, x2 * cos + x1 * sin], axis=-1).astype(x.dtype)


def _to_chunks(t, L):
    B, T, H = t.shape[:3]
    t = t.reshape((B, T // L, L, H) + t.shape[3:])
    return jnp.moveaxis(t, (1, 3), (0, 2))


def _from_chunks(t):
    t = jnp.moveaxis(t, (0, 2), (1, 3))
    B, n, L, H, d = t.shape
    return t.reshape(B, n * L, H, d)


def mlstm_chunkwise(q, k, v, ig, lf, C0, n0, m0):
    T = q.shape[1]
    L = min(CHUNK, T)
    causal = jnp.tril(jnp.ones((L, L), dtype=bool))

    def step(carry, inp):
        C, n, m = carry
        qc, kc, vc, igc, lfc = inp
        b = jnp.cumsum(lfc, axis=-1)
        d_mat = jnp.where(causal, b[..., :, None] - b[..., None, :] + igc[..., None, :], -jnp.inf)
        m_inter = b + m[..., None]
        m_t = jnp.maximum(m_inter, jnp.max(d_mat, axis=-1))
        w_inter = jnp.exp(m_inter - m_t)
        s = jnp.einsum('bhtd,bhsd->bhts', qc, kc) * jnp.exp(d_mat - m_t[..., None])
        num = w_inter[..., None] * jnp.einsum('bhvd,bhtd->bhtv', C, qc) + jnp.einsum('bhts,bhsv->bhtv', s, vc)
        den = w_inter * jnp.einsum('bhd,bhtd->bht', n, qc) + jnp.sum(s, axis=-1)
        h = num / jnp.maximum(jnp.abs(den), jnp.exp(-m_t))[..., None]
        m_new = m_t[..., -1]
        w_k = jnp.exp(b[..., -1:] - b + igc - m_new[..., None])
        decay = jnp.exp(b[..., -1] + m - m_new)
        C_new = decay[..., None, None] * C + jnp.einsum('bhsv,bhsd->bhvd', vc * w_k[..., None], kc)
        n_new = decay[..., None] * n + jnp.einsum('bhs,bhsd->bhd', w_k, kc)
        return (C_new, n_new, m_new), h

    xs = tuple(_to_chunks(t, L) for t in (q, k, v, ig, lf))
    (C, n, m), h = lax.scan(step, (C0, n0, m0), xs)
    return _from_chunks(h), C, n, m


def retention_chunkwise(q, k, v, S0):
    T = q.shape[1]
    L = min(CHUNK, T)
    log_g = jnp.log(1.0 - 2.0 ** (-5.0 - jnp.arange(RET_HEADS, dtype=jnp.float32)))
    idx = jnp.arange(L, dtype=jnp.float32)
    rel = idx[:, None] - idx[None, :]
    intra = jnp.where(rel >= 0, jnp.exp(log_g[:, None, None] * jnp.maximum(rel, 0.0)), 0.0)
    q_dec = jnp.exp(log_g[:, None] * (idx + 1.0))
    k_dec = jnp.exp(log_g[:, None] * (L - 1.0 - idx))
    s_dec = jnp.exp(log_g * L)

    def step(S, inp):
        qc, kc, vc = inp
        a = jnp.einsum('bhtd,bhsd->bhts', qc, kc) * intra
        o = jnp.einsum('bhts,bhsv->bhtv', a, vc) + q_dec[..., None] * jnp.einsum('bhtd,bhdv->bhtv', qc, S)
        S_new = s_dec[:, None, None] * S + jnp.einsum('bhsd,bhsv->bhdv', kc * k_dec[..., None], vc)
        return S_new, o

    xs = tuple(_to_chunks(t, L) for t in (q, k, v))
    S, o = lax.scan(step, S0, xs)
    return _from_chunks(o), S


def mixer_ab(x, pos, C0, n0, m0, S0, w_in, b_if, g_ml, g_ret, w_out):
    B, T, _ = x.shape
    f32 = jnp.float32
    mq, mk, mv, mo, mif, rq, rk, rv, rg = _split(x @ w_in, SPLIT_A)
    heads = lambda t, nh: t.astype(f32).reshape(B, T, nh, -1)
    gates = mif.astype(f32) + b_if.astype(f32)
    ig = gates[..., :ML_HEADS]
    lf = jax.nn.log_sigmoid(gates[..., ML_HEADS:])
    h, C, n, m = mlstm_chunkwise(heads(mq, ML_HEADS), heads(mk, ML_HEADS) * ML_DH ** -0.5,
                                 heads(mv, ML_HEADS), ig, lf,
                                 C0.astype(f32), n0.astype(f32), m0.astype(f32))
    h = head_norm(h, g_ml) * jax.nn.sigmoid(heads(mo, ML_HEADS))
    q_r = rope(heads(rq, RET_HEADS), pos)
    k_r = rope(heads(rk, RET_HEADS), pos) * RET_DK ** -0.5
    o, S = retention_chunkwise(q_r, k_r, heads(rv, RET_HEADS), S0.astype(f32))
    o = head_norm(o, g_ret) * jax.nn.silu(heads(rg, RET_HEADS))
    mixed = jnp.concatenate([h.reshape(B, T, ML_W), o.reshape(B, T, RET_VW)], axis=-1).astype(x.dtype)
    return mixed @ w_out, C, n, m, S


def mixer_c(x, pos, ckv_past, krope_past, w_in, g_q, g_kv, w_uq, w_ukv, w_out):
    B, T, _ = x.shape
    cq, ckv, kr = _split(x @ w_in, (Q_LORA, KV_LORA, QK_ROPE))
    cq = rms_norm(cq, g_q)
    ckv = rms_norm(ckv, g_kv)
    q = (cq @ w_uq).reshape(B, T, MLA_HEADS, QK_NOPE + QK_ROPE)
    q_nope = q[..., :QK_NOPE]
    q_rope = rope(q[..., QK_NOPE:], pos)
    k_rope = rope(kr[:, :, None, :], pos)[:, :, 0, :]
    if ckv_past is None:
        ckv_all, kr_all, k_pos = ckv, k_rope, pos
    else:
        P = ckv_past.shape[1]
        ckv_all = jnp.concatenate([ckv_past.astype(ckv.dtype), ckv], axis=1)
        kr_all = jnp.concatenate([krope_past.astype(k_rope.dtype), k_rope], axis=1)
        k_pos = jnp.arange(P + T, dtype=jnp.int32)
    S = ckv_all.shape[1]
    kv = (ckv_all @ w_ukv).reshape(B, S, MLA_HEADS, QK_NOPE + V_HEAD)
    k_nope, v = kv[..., :QK_NOPE], kv[..., QK_NOPE:]
    k_chunk = k_pos // CHUNK
    qb = min(ATT_QBLOCK, T)
    nb = T // qb

    def block(args):
        qn, qr, qp = args
        s = jnp.einsum('bqhd,bkhd->bhqk', qn, k_nope) + jnp.einsum('bqhr,bkr->bhqk', qr, kr_all)
        s = s.astype(jnp.float32) * MLA_SCALE
        vis = k_chunk[None, :] <= (qp // CHUNK)[:, None]
        p = jax.nn.softmax(jnp.where(vis, s, -jnp.inf), axis=-1).astype(v.dtype)
        return jnp.einsum('bhqk,bkhv->bqhv', p, v)

    to_blocks = lambda t: jnp.swapaxes(t.reshape((B, nb, qb) + t.shape[2:]), 0, 1)
    o = lax.map(block, (to_blocks(q_nope), to_blocks(q_rope), pos.reshape(nb, qb)))
    o = jnp.swapaxes(o, 0, 1).reshape(B, T, MLA_HEADS * V_HEAD)
    return o @ w_out, ckv, k_rope


def swiglu(x, w_gu, w_down):
    g, u = jnp.split(x @ w_gu, 2, axis=-1)
    return (jax.nn.silu(g) * u) @ w_down


def _nrm(key, shape, scale):
    return jax.random.normal(key, shape, jnp.float32) * scale


def setup_inputs(seed: int = 0) -> dict:
    key = jax.random.key(seed)
    ks = jax.random.split(key, 32)
    NA, NC = N_A_LAYERS, N_C_LAYERS
    d = {}
    d['x_prompt'] = _nrm(ks[0], (BATCH, SEQ, D_MODEL), 1.0)
    d['x_sample'] = _nrm(ks[1], (DEC_BATCH, DEC_SEQ, D_MODEL), 1.0)
    d['state_mlstm_C'] = _nrm(ks[2], (NA, DEC_BATCH, ML_HEADS, ML_DH, ML_DH), ML_DH ** -0.5)
    d['state_mlstm_n'] = _nrm(ks[3], (NA, DEC_BATCH, ML_HEADS, ML_DH), ML_DH ** -0.5)
    d['state_mlstm_m'] = _nrm(ks[4], (NA, DEC_BATCH, ML_HEADS), 1.0)
    d['state_ret_S'] = _nrm(ks[5], (NA, DEC_BATCH, RET_HEADS, RET_DK, RET_DV), 1.0)
    d['cache_ckv'] = _nrm(ks[6], (NC, DEC_BATCH, PAST_LEN, KV_LORA), 1.0)
    d['cache_krope'] = _nrm(ks[7], (NC, DEC_BATCH, PAST_LEN, QK_ROPE), 1.0)
    d['w_in_a'] = _nrm(ks[8], (NA, D_MODEL, PROJ_A), D_MODEL ** -0.5)
    d['b_if_a'] = jnp.concatenate([
        _nrm(ks[9], (NA, ML_HEADS), 0.1),
        jnp.linspace(3.0, 6.0, ML_HEADS, dtype=jnp.float32)[None, :] + _nrm(ks[10], (NA, ML_HEADS), 0.1)], axis=-1)
    d['g_ml'] = 1.0 + _nrm(ks[11], (NA, ML_W), 0.02)
    d['g_ret'] = 1.0 + _nrm(ks[12], (NA, RET_VW), 0.02)
    d['w_out_a'] = _nrm(ks[13], (NA, ML_W + RET_VW, D_MODEL), BETA * (ML_W + RET_VW) ** -0.5)
    d['w_in_c'] = _nrm(ks[14], (NC, D_MODEL, PROJ_C), D_MODEL ** -0.5)
    d['g_q'] = 1.0 + _nrm(ks[15], (NC, Q_LORA), 0.02)
    d['g_kv'] = 1.0 + _nrm(ks[16], (NC, KV_LORA), 0.02)
    d['w_uq'] = _nrm(ks[17], (NC, Q_LORA, MLA_HEADS * (QK_NOPE + QK_ROPE)), Q_LORA ** -0.5)
    d['w_ukv'] = _nrm(ks[18], (NC, KV_LORA, MLA_HEADS * (QK_NOPE + V_HEAD)), KV_LORA ** -0.5)
    d['w_out_c'] = _nrm(ks[19], (NC, MLA_HEADS * V_HEAD, D_MODEL), BETA * (MLA_HEADS * V_HEAD) ** -0.5)
    d['ln_mix_g'] = 1.0 + _nrm(ks[20], (DEPTH, D_MODEL), 0.02)
    d['ln_mix_b'] = _nrm(ks[21], (DEPTH, D_MODEL), 0.02)
    d['ln_ffn_g'] = 1.0 + _nrm(ks[22], (DEPTH, D_MODEL), 0.02)
    d['ln_ffn_b'] = _nrm(ks[23], (DEPTH, D_MODEL), 0.02)
    d['w_gu'] = _nrm(ks[24], (DEPTH, D_MODEL, 2 * D_FF), D_MODEL ** -0.5)
    d['w_down'] = _nrm(ks[25], (DEPTH, D_FF, D_MODEL), BETA * D_FF ** -0.5)
    return d


def reference(x_prompt, x_sample, state_mlstm_C, state_mlstm_n, state_mlstm_m, state_ret_S,
              cache_ckv, cache_krope, w_in_a, b_if_a, g_ml, g_ret, w_out_a,
              w_in_c, g_q, g_kv, w_uq, w_ukv, w_out_c,
              ln_mix_g, ln_mix_b, ln_ffn_g, ln_ffn_b, w_gu, w_down):
    f32 = jnp.float32
    Bp, Tp, _ = x_prompt.shape
    Bs, Ts, _ = x_sample.shape
    past = cache_ckv.shape[2]
    pos_p = jnp.arange(Tp, dtype=jnp.int32)
    pos_s = past + jnp.arange(Ts, dtype=jnp.int32)
    xp, xs = x_prompt, x_sample
    pC, pn, pm, pS, pckv, pkr = [], [], [], [], [], []
    sC, sn, sm, sS, sckv, skr = [], [], [], [], [], []
    for layer in range(DEPTH):
        li = layer // 2
        if layer % 2 == 0:
            mp, C, n, m, S = mixer_ab(
                xp, pos_p,
                jnp.zeros((Bp, ML_HEADS, ML_DH, ML_DH), f32), jnp.zeros((Bp, ML_HEADS, ML_DH), f32),
                jnp.zeros((Bp, ML_HEADS), f32), jnp.zeros((Bp, RET_HEADS, RET_DK, RET_DV), f32),
                w_in_a[li], b_if_a[li], g_ml[li], g_ret[li], w_out_a[li])
            pC.append(C); pn.append(n); pm.append(m); pS.append(S)
            ms, C, n, m, S = mixer_ab(
                xs, pos_s, state_mlstm_C[li], state_mlstm_n[li], state_mlstm_m[li], state_ret_S[li],
                w_in_a[li], b_if_a[li], g_ml[li], g_ret[li], w_out_a[li])
            sC.append(C); sn.append(n); sm.append(m); sS.append(S)
        else:
            mp, ckv, kr = mixer_c(xp, pos_p, None, None, w_in_c[li], g_q[li], g_kv[li],
                                  w_uq[li], w_ukv[li], w_out_c[li])
            pckv.append(ckv); pkr.append(kr)
            ms, ckv, kr = mixer_c(xs, pos_s, cache_ckv[li], cache_krope[li], w_in_c[li], g_q[li], g_kv[li],
                                  w_uq[li], w_ukv[li], w_out_c[li])
            sckv.append(ckv); skr.append(kr)
        xp = layer_norm(ALPHA * xp + mp, ln_mix_g[layer], ln_mix_b[layer])
        xs = layer_norm(ALPHA * xs + ms, ln_mix_g[layer], ln_mix_b[layer])
        xp = layer_norm(ALPHA * xp + swiglu(xp, w_gu[layer], w_down[layer]), ln_ffn_g[layer], ln_ffn_b[layer])
        xs = layer_norm(ALPHA * xs + swiglu(xs, w_gu[layer], w_down[layer]), ln_ffn_g[layer], ln_ffn_b[layer])
    return (xp, xs,
            jnp.stack(pC), jnp.stack(pn), jnp.stack(pm), jnp.stack(pS), jnp.stack(pckv), jnp.stack(pkr),
            jnp.stack(sC), jnp.stack(sn), jnp.stack(sm), jnp.stack(sS), jnp.stack(sckv), jnp.stack(skr))
```

```python
import functools
import math

import jax
import jax.numpy as jnp
from jax import lax
from jax.experimental import pallas as pl
from jax.experimental.pallas import tpu as pltpu

F32 = jnp.float32
BF16 = jnp.bfloat16

DEPTH = 2
CHUNK = 64
ML_HEADS = 4
RET_HEADS = 4
MLA_HEADS = 8
QK_NOPE = 128
QK_ROPE = 64
V_HEAD = 128
Q_LORA = 512
KV_LORA = 256
ROPE_BASE = 10000.0
LN_EPS = 1e-5
RMS_EPS = 1e-6
ALPHA = (2 * DEPTH) ** 0.25
MLA_SCALE = (QK_NOPE + QK_ROPE) ** -0.5

LANES = 128
SUBLANES_BF16 = 16
VMEM_LIMIT = 48 * 1024 * 1024

TOKEN_TILE = 512
SCAN_CHUNK = 64
SCAN_TILE = 512
FF_CHUNK = 1408
ATT_TQ = 256
ATT_TK = 512


def _largest_tile(n, cap, mult):
    best = None
    for t in range(mult, min(n, cap) + 1, mult):
        if n % t == 0:
            best = t
    assert best is not None, (n, cap, mult)
    return best


def _params(*sem):
    return pltpu.CompilerParams(dimension_semantics=sem, vmem_limit_bytes=VMEM_LIMIT)


def _resident(shape):
    nd = len(shape)
    return pl.BlockSpec(shape, lambda *_: (0,) * nd, pipeline_mode=pl.Buffered(1))


def _dot(a, b):
    return jnp.dot(a, b, preferred_element_type=F32)


def _dot_nt(a, b):
    return lax.dot_general(a, b, (((1,), (1,)), ((), ())), preferred_element_type=F32)


def _dot_tn(a, b):
    return lax.dot_general(a, b, (((0,), (0,)), ((), ())), preferred_element_type=F32)


def _layer_norm(y, g, b):
    mu = jnp.mean(y, axis=-1, keepdims=True)
    d = y - mu
    var = jnp.mean(d * d, axis=-1, keepdims=True)
    return d * lax.rsqrt(var + LN_EPS) * g + b


def _head_norm(y, g):
    mu = jnp.mean(y, axis=-1, keepdims=True)
    d = y - mu
    var = jnp.mean(d * d, axis=-1, keepdims=True)
    return d * lax.rsqrt(var + LN_EPS) * g


def _rms_norm(y, g):
    return y * lax.rsqrt(jnp.mean(y * y, axis=-1, keepdims=True) + RMS_EPS) * g


def _sigmoid(x):
    return 1.0 / (1.0 + jnp.exp(-x))


def _log_sigmoid(x):
    return jnp.minimum(x, 0.0) - jnp.log1p(jnp.exp(-jnp.abs(x)))


def _split3(x):
    x1 = x.astype(BF16)
    r1 = x - x1.astype(F32)
    x2 = r1.astype(BF16)
    x3 = (r1 - x2.astype(F32)).astype(BF16)
    return x1, x2, x3


def _rope(x, cos, sin):
    w = x.shape[-1]
    reps = w // LANES
    if reps > 1:
        cos = jnp.concatenate([cos] * reps, axis=-1)
        sin = jnp.concatenate([sin] * reps, axis=-1)
    elif w < LANES:
        cos = cos[:, :w]
        sin = sin[:, :w]
    half = QK_ROPE // 2
    lane = lax.broadcasted_iota(jnp.int32, x.shape, 1)
    upper = pltpu.roll(x, w - half, axis=1)
    lower = pltpu.roll(x, half, axis=1)
    partner = jnp.where((lane % QK_ROPE) < half, upper, lower)
    return x * cos + partner * sin


def _rope_table_kernel(cos_ref, sin_ref, *, pos0, period, rows):
    i = pl.program_id(0)
    half = QK_ROPE // 2
    r = lax.broadcasted_iota(jnp.int32, (rows, LANES), 0) + i * rows
    lane = lax.broadcasted_iota(jnp.int32, (rows, LANES), 1)
    pos = (pos0 + r % period).astype(F32)
    j = (lane % half).astype(F32)
    inv = jnp.exp(j * (-math.log(ROPE_BASE) / half))
    ang = pos * inv
    sign = jnp.where((lane % QK_ROPE) < half, -1.0, 1.0)
    cos_ref[...] = jnp.cos(ang)
    sin_ref[...] = sign * jnp.sin(ang)


def _rope_tables(n_rows, pos0, period):
    rows = _largest_tile(n_rows, 512, 8)
    kern = functools.partial(_rope_table_kernel, pos0=pos0, period=period, rows=rows)
    spec = pl.BlockSpec((rows, LANES), lambda i: (i, 0))
    return pl.pallas_call(
        kern, grid=(n_rows // rows,), out_specs=[spec, spec],
        out_shape=[jax.ShapeDtypeStruct((n_rows, LANES), F32)] * 2,
        compiler_params=_params("parallel"), name="rope_tables")()


ML_W = 512
RET_KW = 256
RET_VW = 512
PB_W = 3 * ML_W + 2 * RET_KW + RET_VW
PF_W = ML_W + RET_VW + LANES
OFF_MQ, OFF_MK, OFF_MV = 0, ML_W, 2 * ML_W
OFF_RQ = 3 * ML_W
OFF_RK = OFF_RQ + RET_KW
OFF_RV = OFF_RK + RET_KW
OFF_MO, OFF_RG, OFF_GATE = 0, ML_W, ML_W + RET_VW


def _proj_a_kernel(x_ref, w_ref, wg_ref, bcol_ref, brow_ref, cos_ref, sin_ref, pb_ref, pf_ref, gr_ref, *, chunk):
    x = x_ref[...].astype(BF16)
    cos = cos_ref[...]
    sin = sin_ref[...]
    ml_dh = ML_W // ML_HEADS
    ret_dk = RET_KW // RET_HEADS
    pb_ref[:, OFF_MQ:OFF_MQ + ML_W] = _dot(x, w_ref[:, OFF_MQ:OFF_MQ + ML_W]).astype(BF16)
    pb_ref[:, OFF_MK:OFF_MK + ML_W] = (_dot(x, w_ref[:, OFF_MK:OFF_MK + ML_W]) * ml_dh ** -0.5).astype(BF16)
    pb_ref[:, OFF_MV:OFF_MV + ML_W] = _dot(x, w_ref[:, OFF_MV:OFF_MV + ML_W]).astype(BF16)
    rq = _dot(x, w_ref[:, OFF_RQ:OFF_RQ + RET_KW])
    pb_ref[:, OFF_RQ:OFF_RQ + RET_KW] = _rope(rq, cos, sin).astype(BF16)
    rk = _dot(x, w_ref[:, OFF_RK:OFF_RK + RET_KW])
    pb_ref[:, OFF_RK:OFF_RK + RET_KW] = (_rope(rk, cos, sin) * ret_dk ** -0.5).astype(BF16)
    pb_ref[:, OFF_RV:OFF_RV + RET_VW] = _dot(x, w_ref[:, OFF_RV:OFF_RV + RET_VW]).astype(BF16)
    f0 = PB_W
    pf_ref[:, OFF_MO:OFF_MO + ML_W] = _dot(x, w_ref[:, f0 + OFF_MO:f0 + OFF_MO + ML_W])
    pf_ref[:, OFF_RG:OFF_RG + RET_VW] = _dot(x, w_ref[:, f0 + OFF_RG:f0 + OFF_RG + RET_VW])
    pf_ref[:, OFF_GATE:OFF_GATE + LANES] = _dot(x, w_ref[:, f0 + OFF_GATE:f0 + OFF_GATE + LANES]) + bcol_ref[...]
    g_row = _dot_nt(wg_ref[...], x) + brow_ref[...]
    for c in range(x.shape[0] // chunk):
        gr_ref[c] = g_row[:, c * chunk:(c + 1) * chunk]


def _proj_a(x2d, w_all, w_gate_t, b_col, b_row, cos, sin, seq_len, chunk):
    n, d = x2d.shape
    tm = min(TOKEN_TILE, n)
    tab_blocks = cos.shape[0] // tm
    kern = functools.partial(_proj_a_kernel, chunk=chunk)
    tab_spec = pl.BlockSpec((tm, LANES), lambda i: (i % tab_blocks, 0))
    return pl.pallas_call(
        kern, grid=(n // tm,),
        in_specs=[pl.BlockSpec((tm, d), lambda i: (i, 0)),
                  _resident(w_all.shape), _resident(w_gate_t.shape), _resident(b_col.shape), _resident(b_row.shape),
                  tab_spec, tab_spec],
        out_specs=[pl.BlockSpec((tm, PB_W), lambda i: (i, 0)),
                   pl.BlockSpec((tm, PF_W), lambda i: (i, 0)),
                   pl.BlockSpec((tm // chunk, 2 * ML_HEADS, chunk), lambda i: (i, 0, 0))],
        out_shape=[jax.ShapeDtypeStruct((n, PB_W), BF16),
                   jax.ShapeDtypeStruct((n, PF_W), F32),
                   jax.ShapeDtypeStruct((n // chunk, 2 * ML_HEADS, chunk), F32)],
        compiler_params=_params("parallel"), name="proj_a")(x2d, w_all, w_gate_t, b_col, b_row, cos, sin)


def _scan_kernel(pb_ref, pf_ref, gr_ref, gml_ref, gret_ref, c0_ref, n0_ref, m0_ref, s0_ref,
                 mix_ref, c_ref, n_ref, m_ref, s_ref, *, chunk, n_chunks):
    L = chunk
    ml_dh = ML_W // ML_HEADS
    ret_dk = RET_KW // RET_HEADS
    ret_dv = RET_VW // RET_HEADS

    @pl.when(pl.program_id(1) == 0)
    def _():
        c_ref[...] = c0_ref[...]
        n_ref[...] = n0_ref[...]
        m_ref[...] = m0_ref[...]
        s_ref[...] = s0_ref[...]

    row = lax.broadcasted_iota(jnp.int32, (L, L), 0)
    col = lax.broadcasted_iota(jnp.int32, (L, L), 1)
    causal = col <= row
    tril = jnp.where(causal, 1.0, 0.0).astype(BF16)
    triu = jnp.where(row <= col, 1.0, 0.0).astype(BF16)
    rel = (row - col).astype(F32)
    t_col = lax.broadcasted_iota(jnp.int32, (L, 1), 0).astype(F32)
    log_g = [math.log(1.0 - 2.0 ** (-5.0 - h)) for h in range(RET_HEADS)]

    def body(j, carry):
        r0 = pl.multiple_of(j * L, L)
        rows = pl.ds(r0, L)
        gc = pf_ref[rows, OFF_GATE:OFF_GATE + LANES]
        gr = gr_ref[j]
        lc1, lc2, lc3 = _split3(_log_sigmoid(gc))
        b_c = _dot(tril, lc1) + _dot(tril, lc2) + _dot(tril, lc3)
        lr1, lr2, lr3 = _split3(_log_sigmoid(gr))
        b_r = _dot(lr1, triu) + _dot(lr2, triu) + _dot(lr3, triu)

        for h in range(ML_HEADS):
            q = pb_ref[rows, OFF_MQ + h * ml_dh:OFF_MQ + (h + 1) * ml_dh]
            k = pb_ref[rows, OFF_MK + h * ml_dh:OFF_MK + (h + 1) * ml_dh]
            v = pb_ref[rows, OFF_MV + h * ml_dh:OFF_MV + (h + 1) * ml_dh]
            b_col = b_c[:, ML_HEADS + h:ML_HEADS + h + 1]
            a_col = gc[:, h:h + 1] - b_col
            a_row = gr[h:h + 1, :] - b_r[ML_HEADS + h:ML_HEADS + h + 1, :]
            m_old = m_ref[0, h]
            c_old = c_ref[0, h]
            n_old = n_ref[0, h]
            dmat = jnp.where(causal, a_row, -jnp.inf)
            g = jnp.maximum(m_old, jnp.max(dmat, axis=1, keepdims=True))
            e = jnp.exp(dmat - g)
            s = _dot_nt(q, k) * e
            w_inter = jnp.exp(m_old - g)
            num = w_inter * _dot_nt(q, c_old.astype(BF16)) + _dot(s.astype(BF16), v)
            qn = jnp.sum(q.astype(F32) * n_old, axis=1, keepdims=True)
            den = w_inter * qn + jnp.sum(s, axis=1, keepdims=True)
            hh = num / jnp.maximum(jnp.abs(den), jnp.exp(-(b_col + g)))
            gain = gml_ref[:, h * ml_dh:(h + 1) * ml_dh]
            gate = pf_ref[rows, OFF_MO + h * ml_dh:OFF_MO + (h + 1) * ml_dh]
            mix_ref[rows, h * ml_dh:(h + 1) * ml_dh] = (_head_norm(hh, gain) * _sigmoid(gate)).astype(BF16)
            g_last = g[L - 1:L, :]
            w_k = jnp.exp(a_col - g_last)
            decay = jnp.exp(m_old - g_last)
            vw = (v.astype(F32) * w_k).astype(BF16)
            c_ref[0, h] = decay * c_old + _dot_tn(vw, k)
            n_ref[0, h] = decay * n_old + jnp.sum(k.astype(F32) * w_k, axis=0, keepdims=True)
            m_ref[0, h] = b_col[L - 1:L, :] + g_last

        for h in range(RET_HEADS):
            q = pb_ref[rows, OFF_RQ + h * ret_dk:OFF_RQ + (h + 1) * ret_dk]
            k = pb_ref[rows, OFF_RK + h * ret_dk:OFF_RK + (h + 1) * ret_dk]
            v = pb_ref[rows, OFF_RV + h * ret_dv:OFF_RV + (h + 1) * ret_dv]
            s_old = s_ref[0, h]
            intra = jnp.where(causal, jnp.exp(log_g[h] * jnp.maximum(rel, 0.0)), 0.0)
            q_dec = jnp.exp(log_g[h] * (t_col + 1.0))
            k_dec = jnp.exp(log_g[h] * (L - 1.0 - t_col))
            s_dec = math.exp(log_g[h] * L)
            a = _dot_nt(q, k) * intra
            o = _dot(a.astype(BF16), v) + q_dec * _dot(q, s_old.astype(BF16))
            kd = (k.astype(F32) * k_dec).astype(BF16)
            s_ref[0, h] = s_dec * s_old + _dot_tn(kd, v)
            gain = gret_ref[:, h * ret_dv:(h + 1) * ret_dv]
            gate = pf_ref[rows, OFF_RG + h * ret_dv:OFF_RG + (h + 1) * ret_dv]
            o = _head_norm(o, gain) * (gate * _sigmoid(gate))
            mix_ref[rows, ML_W + h * ret_dv:ML_W + (h + 1) * ret_dv] = o.astype(BF16)
        return carry

    lax.fori_loop(0, n_chunks, body, 0)


def _scan(pb, pf, gr, g_ml, g_ret, c0, n0, m0, s0, batch, seq_len, chunk):
    ts = min(SCAN_TILE, seq_len)
    steps = seq_len // ts
    n_chunks = ts // chunk
    kern = functools.partial(_scan_kernel, chunk=chunk, n_chunks=n_chunks)
    tok = lambda b, c: (b * steps + c, 0)
    state = lambda shape: pl.BlockSpec((1,) + shape[1:], lambda b, c: (b, 0, 0, 0))
    n = batch * seq_len
    return pl.pallas_call(
        kern, grid=(batch, steps),
        in_specs=[pl.BlockSpec((ts, PB_W), tok), pl.BlockSpec((ts, PF_W), tok),
                  pl.BlockSpec((n_chunks, 2 * ML_HEADS, chunk), lambda b, c: (b * steps + c, 0, 0)),
                  _resident(g_ml.shape), _resident(g_ret.shape),
                  state(c0.shape), state(n0.shape), state(m0.shape), state(s0.shape)],
        out_specs=[pl.BlockSpec((ts, ML_W + RET_VW), tok),
                   state(c0.shape), state(n0.shape), state(m0.shape), state(s0.shape)],
        out_shape=[jax.ShapeDtypeStruct((n, ML_W + RET_VW), BF16),
                   jax.ShapeDtypeStruct(c0.shape, F32), jax.ShapeDtypeStruct(n0.shape, F32),
                   jax.ShapeDtypeStruct(m0.shape, F32), jax.ShapeDtypeStruct(s0.shape, F32)],
        compiler_params=_params("parallel", "arbitrary"), name="mlstm_ret_scan",
    )(pb, pf, gr, g_ml, g_ret, c0, n0, m0, s0)


def _out_ln_kernel(x_ref, a_ref, w_ref, g_ref, b_ref, y_ref):
    y = ALPHA * x_ref[...] + _dot(a_ref[...], w_ref[...])
    y_ref[...] = _layer_norm(y, g_ref[...], b_ref[...])


def _out_ln(x2d, a2d, w, g, b):
    n, d = x2d.shape
    tm = min(TOKEN_TILE, n)
    return pl.pallas_call(
        _out_ln_kernel, grid=(n // tm,),
        in_specs=[pl.BlockSpec((tm, d), lambda i: (i, 0)), pl.BlockSpec((tm, a2d.shape[1]), lambda i: (i, 0)),
                  _resident(w.shape), _resident(g.shape), _resident(b.shape)],
        out_specs=pl.BlockSpec((tm, d), lambda i: (i, 0)),
        out_shape=jax.ShapeDtypeStruct((n, d), F32),
        compiler_params=_params("parallel"), name="out_proj_ln")(x2d, a2d, w, g, b)


def _ffn_kernel(x_ref, wgu_ref, wd_ref, g_ref, b_ref, y_ref, *, d_ff):
    x = x_ref[...]
    xb = x.astype(BF16)
    acc = ALPHA * x
    for c0 in range(0, d_ff, FF_CHUNK):
        gate = _dot(xb, wgu_ref[:, c0:c0 + FF_CHUNK])
        up = _dot(xb, wgu_ref[:, d_ff + c0:d_ff + c0 + FF_CHUNK])
        act = (gate * _sigmoid(gate) * up).astype(BF16)
        acc = acc + _dot(act, wd_ref[c0:c0 + FF_CHUNK, :])
    y_ref[...] = _layer_norm(acc, g_ref[...], b_ref[...])


def _ffn(x2d, w_gu, w_down, g, b):
    n, d = x2d.shape
    d_ff = w_down.shape[0]
    assert d_ff % FF_CHUNK == 0
    tm = min(TOKEN_TILE, n)
    kern = functools.partial(_ffn_kernel, d_ff=d_ff)
    return pl.pallas_call(
        kern, grid=(n // tm,),
        in_specs=[pl.BlockSpec((tm, d), lambda i: (i, 0)),
                  _resident(w_gu.shape), _resident(w_down.shape), _resident(g.shape), _resident(b.shape)],
        out_specs=pl.BlockSpec((tm, d), lambda i: (i, 0)),
        out_shape=jax.ShapeDtypeStruct((n, d), F32),
        compiler_params=_params("parallel"), name="swiglu_ln")(x2d, w_gu, w_down, g, b)


QK_PAD = 256


def _proj_c_kernel(x_ref, w_ref, gq_ref, gkv_ref, wuq_ref, cos_ref, sin_ref, q_ref, ckv_ref, kr_ref):
    x = x_ref[...].astype(BF16)
    cos = cos_ref[...]
    sin = sin_ref[...]
    cq = _rms_norm(_dot(x, w_ref[:, 0:Q_LORA]), gq_ref[...])
    ckv_ref[...] = _rms_norm(_dot(x, w_ref[:, Q_LORA:Q_LORA + KV_LORA]), gkv_ref[...])
    kr = _dot(x, w_ref[:, Q_LORA + KV_LORA:Q_LORA + KV_LORA + LANES])
    kr_ref[...] = _rope(kr, cos, sin)[:, :QK_ROPE]
    cqb = cq.astype(BF16)
    lane = lax.broadcasted_iota(jnp.int32, cos.shape, 1)
    cos_r = jnp.where(lane < QK_ROPE, cos, 1.0)
    sin_r = jnp.where(lane < QK_ROPE, sin, 0.0)
    for h in range(MLA_HEADS):
        qh = _dot(cqb, wuq_ref[:, h * QK_PAD:(h + 1) * QK_PAD])
        q_ref[0, h, :, 0:QK_NOPE] = qh[:, 0:QK_NOPE].astype(BF16)
        q_ref[0, h, :, QK_NOPE:QK_PAD] = _rope(qh[:, QK_NOPE:QK_PAD], cos_r, sin_r).astype(BF16)


def _proj_c(x2d, w_in, g_q, g_kv, w_uq, cos, sin, batch, seq_len):
    n, d = x2d.shape
    tm = _largest_tile(seq_len, TOKEN_TILE, SUBLANES_BF16)
    steps = seq_len // tm
    tab_spec = pl.BlockSpec((tm, LANES), lambda b, i: (i, 0))
    tok = lambda b, i: (b * steps + i, 0)
    return pl.pallas_call(
        _proj_c_kernel, grid=(batch, steps),
        in_specs=[pl.BlockSpec((tm, d), tok), _resident(w_in.shape), _resident(g_q.shape), _resident(g_kv.shape),
                  _resident(w_uq.shape), tab_spec, tab_spec],
        out_specs=[pl.BlockSpec((1, MLA_HEADS, tm, QK_PAD), lambda b, i: (b, 0, i, 0)),
                   pl.BlockSpec((tm, KV_LORA), tok), pl.BlockSpec((tm, QK_ROPE), tok)],
        out_shape=[jax.ShapeDtypeStruct((batch, MLA_HEADS, seq_len, QK_PAD), BF16),
                   jax.ShapeDtypeStruct((n, KV_LORA), F32), jax.ShapeDtypeStruct((n, QK_ROPE), F32)],
        compiler_params=_params("parallel", "parallel"), name="proj_c")(x2d, w_in, g_q, g_kv, w_uq, cos, sin)


def _kv_up_kernel(ckv_ref, kr_ref, w_ref, k_ref, v_ref):
    c = ckv_ref[...].astype(BF16)
    n_w = MLA_HEADS * QK_NOPE
    k_nope = _dot(c, w_ref[:, 0:n_w]).astype(BF16)
    v = _dot(c, w_ref[:, n_w:n_w + MLA_HEADS * V_HEAD]).astype(BF16)
    sel = (lax.broadcasted_iota(jnp.int32, (QK_ROPE, LANES), 0)
           == lax.broadcasted_iota(jnp.int32, (QK_ROPE, LANES), 1))
    kr = _dot(kr_ref[...].astype(BF16), jnp.where(sel, 1.0, 0.0).astype(BF16)).astype(BF16)
    for h in range(MLA_HEADS):
        k_ref[0, h, :, 0:QK_NOPE] = k_nope[:, h * QK_NOPE:(h + 1) * QK_NOPE]
        k_ref[0, h, :, QK_NOPE:QK_PAD] = kr
        v_ref[0, h] = v[:, h * V_HEAD:(h + 1) * V_HEAD]


def _kv_up(ckv2d, kr2d, w_ukv, batch, kv_len):
    ts = _largest_tile(kv_len, 1024, SUBLANES_BF16)
    steps = kv_len // ts
    tok = lambda b, i: (b * steps + i, 0)
    return pl.pallas_call(
        _kv_up_kernel, grid=(batch, steps),
        in_specs=[pl.BlockSpec((ts, KV_LORA), tok), pl.BlockSpec((ts, QK_ROPE), tok), _resident(w_ukv.shape)],
        out_specs=[pl.BlockSpec((1, MLA_HEADS, ts, QK_PAD), lambda b, i: (b, 0, i, 0)),
                   pl.BlockSpec((1, MLA_HEADS, ts, V_HEAD), lambda b, i: (b, 0, i, 0))],
        out_shape=[jax.ShapeDtypeStruct((batch, MLA_HEADS, kv_len, QK_PAD), BF16),
                   jax.ShapeDtypeStruct((batch, MLA_HEADS, kv_len, V_HEAD), BF16)],
        compiler_params=_params("parallel", "parallel"), name="kv_up")(ckv2d, kr2d, w_ukv)


def _attn_kernel(q_ref, k_ref, v_ref, o_ref, *, q_pos0, tq, tk):
    t_len = q_ref.shape[2]
    s_len = k_ref.shape[2]

    def q_tile(i, carry):
        q0 = pl.multiple_of(i * tq, tq)
        q = q_ref[0, 0, pl.ds(q0, tq), :]
        limit = jnp.minimum(((q_pos0 + q0 + tq - 1) // CHUNK + 1) * CHUNK, s_len)
        n_kv = (limit + tk - 1) // tk
        q_chunk = (q_pos0 + q0 + lax.broadcasted_iota(jnp.int32, (tq, tk), 0)) // CHUNK

        def kv_tile(j, st):
            m, l, acc = st
            k0 = pl.multiple_of(j * tk, tk)
            k = k_ref[0, 0, pl.ds(k0, tk), :]
            v = v_ref[0, 0, pl.ds(k0, tk), :]
            s = _dot_nt(q, k) * MLA_SCALE
            k_chunk = (k0 + lax.broadcasted_iota(jnp.int32, (tq, tk), 1)) // CHUNK
            s = jnp.where(k_chunk <= q_chunk, s, -jnp.inf)
            m_new = jnp.maximum(m, jnp.max(s, axis=1, keepdims=True))
            alpha = jnp.exp(m - m_new)
            p = jnp.exp(s - m_new)
            l = alpha * l + jnp.sum(p, axis=1, keepdims=True)
            acc = alpha * acc + _dot(p.astype(BF16), v)
            return m_new, l, acc

        init = (jnp.full((tq, 1), -jnp.inf, F32), jnp.zeros((tq, 1), F32), jnp.zeros((tq, V_HEAD), F32))
        m, l, acc = lax.fori_loop(0, n_kv, kv_tile, init)
        o_ref[0, pl.ds(q0, tq), :] = (acc / l).astype(BF16)
        return carry

    lax.fori_loop(0, t_len // tq, q_tile, 0)


def _attention(q, k, v, q_pos0):
    batch, heads, t_len, _ = q.shape
    s_len = k.shape[2]
    tq = _largest_tile(t_len, ATT_TQ, CHUNK)
    tk = _largest_tile(s_len, ATT_TK, CHUNK)
    kern = functools.partial(_attn_kernel, q_pos0=q_pos0, tq=tq, tk=tk)
    return pl.pallas_call(
        kern, grid=(batch, heads),
        in_specs=[pl.BlockSpec((1, 1, t_len, QK_PAD), lambda b, h: (b, h, 0, 0)),
                  pl.BlockSpec((1, 1, s_len, QK_PAD), lambda b, h: (b, h, 0, 0)),
                  pl.BlockSpec((1, 1, s_len, V_HEAD), lambda b, h: (b, h, 0, 0))],
        out_specs=pl.BlockSpec((1, t_len, V_HEAD), lambda b, h: (b, 0, h)),
        out_shape=jax.ShapeDtypeStruct((batch, t_len, heads * V_HEAD), BF16),
        compiler_params=_params("parallel", "parallel"), name="mla_attention")(q, k, v)


def _prep_w_in_a(w, b_if):
    sizes = (ML_W, ML_W, ML_W, ML_W, 2 * ML_HEADS, RET_KW, RET_KW, RET_VW, RET_VW)
    offs = [sum(sizes[:i]) for i in range(len(sizes) + 1)]
    mq, mk, mv, mo, mif, rq, rk, rv, rg = [w[:, offs[i]:offs[i + 1]] for i in range(len(sizes))]
    gate_pad = jnp.pad(mif, ((0, 0), (0, LANES - 2 * ML_HEADS)))
    w_all = jnp.concatenate([mq, mk, mv, rq, rk, rv, mo, rg, gate_pad], axis=1).astype(BF16)
    w_gate_t = mif.T.astype(BF16)
    b_col = jnp.pad(b_if, (0, LANES - 2 * ML_HEADS)).reshape(1, LANES).astype(F32)
    b_row = b_if.reshape(2 * ML_HEADS, 1).astype(F32)
    return w_all, w_gate_t, b_col, b_row


def _heads_split(w, widths):
    k = w.shape[0]
    w = w.reshape(k, MLA_HEADS, sum(widths))
    parts, off = [], 0
    for wd in widths:
        parts.append(w[:, :, off:off + wd].reshape(k, MLA_HEADS * wd))
        off += wd
    return jnp.concatenate(parts, axis=1)


def _heads_pad(w, width, padded):
    k = w.shape[0]
    w = w.reshape(k, MLA_HEADS, width)
    return jnp.pad(w, ((0, 0), (0, 0), (0, padded - width))).reshape(k, MLA_HEADS * padded)


def _mixer_a(x, pos0, states, wts, li):
    batch, seq_len, d = x.shape
    n = batch * seq_len
    chunk = min(SCAN_CHUNK, seq_len)
    x2d = x.reshape(n, d)
    tm = min(TOKEN_TILE, n)
    cos, sin = _rope_tables(max(seq_len, tm), pos0, seq_len)
    pb, pf, gr = _proj_a(x2d, wts["w_all"], wts["w_gate_t"], wts["b_col"], wts["b_row"], cos, sin, seq_len, chunk)
    c0, n0, m0, s0 = states
    mixed, c, nn, m, s = _scan(
        pb, pf, gr, wts["g_ml"], wts["g_ret"],
        c0, n0.reshape(batch, ML_HEADS, 1, -1), m0.reshape(batch, ML_HEADS, 1, 1), s0, batch, seq_len, chunk)
    y = _out_ln(x2d, mixed, wts["w_out"], wts["ln_g"], wts["ln_b"])
    return y, (c, nn.reshape(batch, ML_HEADS, -1), m.reshape(batch, ML_HEADS), s)


def _mixer_c(x2d, batch, seq_len, pos0, ckv_past, kr_past, wts):
    cos, sin = _rope_tables(seq_len, pos0, seq_len)
    q, ckv, kr = _proj_c(x2d, wts["w_in"], wts["g_q"], wts["g_kv"], wts["w_uq"], cos, sin, batch, seq_len)
    if ckv_past is None:
        ckv_all, kr_all, kv_len = ckv, kr, seq_len
    else:
        past = ckv_past.shape[1]
        kv_len = past + seq_len
        ckv_all = jnp.concatenate([ckv_past, ckv.reshape(batch, seq_len, -1)], axis=1).reshape(batch * kv_len, -1)
        kr_all = jnp.concatenate([kr_past, kr.reshape(batch, seq_len, -1)], axis=1).reshape(batch * kv_len, -1)
    k, v = _kv_up(ckv_all, kr_all, wts["w_ukv"], batch, kv_len)
    o = _attention(q, k, v, pos0)
    y = _out_ln(x2d, o.reshape(batch * seq_len, -1), wts["w_out"], wts["ln_g"], wts["ln_b"])
    return y, ckv.reshape(batch, seq_len, -1), kr.reshape(batch, seq_len, -1)


def kernel(x_prompt, x_sample, state_mlstm_C, state_mlstm_n, state_mlstm_m, state_ret_S, cache_ckv, cache_krope,
           w_in_a, b_if_a, g_ml, g_ret, w_out_a, w_in_c, g_q, g_kv, w_uq, w_ukv, w_out_c,
           ln_mix_g, ln_mix_b, ln_ffn_g, ln_ffn_b, w_gu, w_down):
    bp, tp, d = x_prompt.shape
    bs, ts, _ = x_sample.shape
    past = cache_ckv.shape[2]
    row = lambda a: a.reshape(1, -1).astype(F32)

    xp = x_prompt.reshape(bp * tp, d)
    xs = x_sample.reshape(bs * ts, d)
    p_state, s_state, p_lat, s_lat = [], [], [], []
    for layer in range(DEPTH):
        li = layer // 2
        if layer % 2 == 0:
            w_all, w_gate_t, b_col, b_row = _prep_w_in_a(w_in_a[li], b_if_a[li])
            wts = dict(w_all=w_all, w_gate_t=w_gate_t, b_col=b_col, b_row=b_row, g_ml=row(g_ml[li]),
                       g_ret=row(g_ret[li]), w_out=w_out_a[li].astype(BF16),
                       ln_g=row(ln_mix_g[layer]), ln_b=row(ln_mix_b[layer]))
            ml_dh = ML_W // ML_HEADS
            zeros = (jnp.zeros((bp, ML_HEADS, ml_dh, ml_dh), F32), jnp.zeros((bp, ML_HEADS, ml_dh), F32),
                     jnp.zeros((bp, ML_HEADS), F32),
                     jnp.zeros((bp, RET_HEADS, RET_KW // RET_HEADS, RET_VW // RET_HEADS), F32))
            xp, st = _mixer_a(xp.reshape(bp, tp, d), 0, zeros, wts, li)
            p_state.append(st)
            xs, st = _mixer_a(xs.reshape(bs, ts, d), past,
                              (state_mlstm_C[li], state_mlstm_n[li], state_mlstm_m[li], state_ret_S[li]), wts, li)
            s_state.append(st)
        else:
            w_in = jnp.pad(w_in_c[li], ((0, 0), (0, LANES - QK_ROPE))).astype(BF16)
            wts = dict(w_in=w_in, g_q=row(g_q[li]), g_kv=row(g_kv[li]),
                       w_uq=_heads_pad(w_uq[li], QK_NOPE + QK_ROPE, QK_PAD).astype(BF16),
                       w_ukv=_heads_split(w_ukv[li], (QK_NOPE, V_HEAD)).astype(BF16),
                       w_out=w_out_c[li].astype(BF16), ln_g=row(ln_mix_g[layer]), ln_b=row(ln_mix_b[layer]))
            xp, ckv, kr = _mixer_c(xp, bp, tp, 0, None, None, wts)
            p_lat.append((ckv, kr))
            xs, ckv, kr = _mixer_c(xs, bs, ts, past, cache_ckv[li], cache_krope[li], wts)
            s_lat.append((ckv, kr))
        w_gu_b = w_gu[layer].astype(BF16)
        w_down_b = w_down[layer].astype(BF16)
        xp = _ffn(xp, w_gu_b, w_down_b, row(ln_ffn_g[layer]), row(ln_ffn_b[layer]))
        xs = _ffn(xs, w_gu_b, w_down_b, row(ln_ffn_g[layer]), row(ln_ffn_b[layer]))

    stack = lambda items, k: jnp.stack([it[k] for it in items])
    return (xp.reshape(bp, tp, d), xs.reshape(bs, ts, d),
            stack(p_state, 0), stack(p_state, 1), stack(p_state, 2), stack(p_state, 3),
            stack(p_lat, 0), stack(p_lat, 1),
            stack(s_state, 0), stack(s_state, 1), stack(s_state, 2), stack(s_state, 3),
            stack(s_lat, 0), stack(s_lat, 1))
```
